```python
import jax, jax.numpy as jnp
from jax import lax
import numpy as np

D_MODEL = 2048
BATCH = 2
SEQ = 16384
DEPTH = 1

D_MIX = D_MODEL
D_ATTN = D_MIX // 2
D_GMLP = D_MIX - D_ATTN
HEAD_DIM = 128
N_HEADS = D_ATTN // HEAD_DIM
N_KV_HEADS = 2
GQA_GROUP = N_HEADS // N_KV_HEADS
WINDOW = 128
ATTN_BLOCK = 128
GMLP_GROUP_DIM = 128
N_GMLP_GROUPS = D_GMLP // GMLP_GROUP_DIM
GMLP_CHUNK = 128
D_FF = 5632
CONV_WIDTH = 3
EPS = 1e-6

Q_COLS = N_HEADS * HEAD_DIM
KV_COLS = N_KV_HEADS * HEAD_DIM
IN_COLS = Q_COLS + 2 * KV_COLS + 2 * D_GMLP

kernel_name = "hymba_style_window_gqa_chunked_gmlp_convffn_encoder"


def rmsnorm(x, g):
    xf = x.astype(jnp.float32)
    y = xf * lax.rsqrt(jnp.mean(xf * xf, axis=-1, keepdims=True) + EPS)
    return (y * g.astype(jnp.float32)).astype(x.dtype)


def alibi_slopes(n):
    return jnp.exp2(-8.0 * jnp.arange(1, n + 1, dtype=jnp.float32) / n)


def windowed_gqa(q, k, v, sink):
    B, S = q.shape[0], q.shape[1]
    nb = S // ATTN_BLOCK
    qb = q.reshape(B, nb, ATTN_BLOCK, N_KV_HEADS, GQA_GROUP, HEAD_DIM)
    pad = ((0, 0), (ATTN_BLOCK, ATTN_BLOCK), (0, 0), (0, 0))

    def band(t):
        tp = jnp.pad(t, pad).reshape(B, nb + 2, ATTN_BLOCK, N_KV_HEADS, HEAD_DIM)
        return jnp.concatenate([tp[:, :-2], tp[:, 1:-1], tp[:, 2:]], axis=2)

    kb, vb = band(k), band(v)
    s = jnp.einsum('bnqkgd,bnskd->bnkgqs', qb, kb).astype(jnp.float32) * (HEAD_DIM ** -0.5)

    qi = jnp.arange(ATTN_BLOCK)[:, None]
    sj = jnp.arange(3 * ATTN_BLOCK)[None, :]
    rel = sj - ATTN_BLOCK - qi
    key_pos = jnp.arange(nb)[:, None, None] * ATTN_BLOCK - ATTN_BLOCK + sj[None]
    valid = (jnp.abs(rel) <= WINDOW)[None] & (key_pos >= 0) & (key_pos < S)
    slopes = alibi_slopes(N_HEADS).reshape(N_KV_HEADS, GQA_GROUP)
    alibi = -slopes[:, :, None, None] * jnp.abs(rel).astype(jnp.float32)
    s = jnp.where(valid[None, :, None, None], s + alibi[None, None], -jnp.inf)

    sink_l = jnp.broadcast_to(
        sink.astype(jnp.float32).reshape(N_KV_HEADS, GQA_GROUP)[None, None, :, :, None, None],
        s.shape[:-1] + (1,))
    p = jax.nn.softmax(jnp.concatenate([s, sink_l], axis=-1), axis=-1)[..., :-1]
    o = jnp.einsum('bnkgqs,bnskd->bnqkgd', p.astype(vb.dtype), vb)
    return o.reshape(B, S, N_HEADS * HEAD_DIM)


def chunked_spatial_gating(z, v_norm_g, ws, b):
    B, S = z.shape[0], z.shape[1]
    z = jax.nn.gelu(z)
    u, v = z[..., :D_GMLP], z[..., D_GMLP:]
    v = v.reshape(B, S // GMLP_CHUNK, GMLP_CHUNK, N_GMLP_GROUPS, GMLP_GROUP_DIM)
    v = rmsnorm(v, v_norm_g.reshape(N_GMLP_GROUPS, GMLP_GROUP_DIM))
    v = jnp.einsum('hts,bnshc->bnthc', ws, v) + b.T[None, None, :, :, None]
    return u * v.reshape(B, S, D_GMLP)


def conv_ffn(h, w_up, conv_w, conv_b, w_down):
    a = h @ w_up
    c = a.shape[-1]
    a = lax.conv_general_dilated(
        a, conv_w[:, None, :].astype(a.dtype), window_strides=(1,),
        padding=((CONV_WIDTH // 2, CONV_WIDTH // 2),),
        dimension_numbers=('NWC', 'WIO', 'NWC'), feature_group_count=c) + conv_b
    g, u = a[..., :D_FF], a[..., D_FF:]
    return (jax.nn.silu(g) * u) @ w_down


def setup_inputs(seed: int = 0) -> dict:
    key = jax.random.key(seed)
    ks = jax.random.split(key, 16)
    f32 = jnp.float32
    nrm = lambda k, shape, s: jax.random.normal(k, shape, f32) * s
    L = DEPTH
    return {
        "x": jax.random.normal(ks[0], (BATCH, SEQ, D_MODEL), f32),
        "norm1_g": 1.0 + nrm(ks[1], (L, D_MODEL), 0.02),
        "w_in": nrm(ks[2], (L, D_MODEL, IN_COLS), D_MODEL ** -0.5),
        "gmlp_v_norm_g": 1.0 + nrm(ks[3], (L, D_GMLP), 0.02),
        "gmlp_ws": nrm(ks[4], (L, N_GMLP_GROUPS, GMLP_CHUNK, GMLP_CHUNK), GMLP_CHUNK ** -0.5),
        "gmlp_b": 1.0 + nrm(ks[5], (L, N_GMLP_GROUPS, GMLP_CHUNK), 0.1),
        "attn_sink": nrm(ks[6], (L, N_HEADS), 0.5),
        "attn_out_norm_g": 1.0 + nrm(ks[7], (L, D_ATTN), 0.02),
        "gmlp_out_norm_g": 1.0 + nrm(ks[8], (L, D_GMLP), 0.02),
        "w_out": nrm(ks[9], (L, D_MIX, D_MODEL), D_MIX ** -0.5),
        "norm2_g": 1.0 + nrm(ks[10], (L, D_MODEL), 0.02),
        "w_up": nrm(ks[11], (L, D_MODEL, 2 * D_FF), D_MODEL ** -0.5),
        "conv_w": nrm(ks[12], (L, CONV_WIDTH, 2 * D_FF), CONV_WIDTH ** -0.5),
        "conv_b": nrm(ks[13], (L, 2 * D_FF), 0.01),
        "w_down": nrm(ks[14], (L, D_FF, D_MODEL), D_FF ** -0.5),
        "final_g": 1.0 + nrm(ks[15], (D_MODEL,), 0.02),
    }


def reference(x, norm1_g, w_in, gmlp_v_norm_g, gmlp_ws, gmlp_b, attn_sink,
              attn_out_norm_g, gmlp_out_norm_g, w_out, norm2_g, w_up, conv_w,
              conv_b, w_down, final_g):
    B, S = x.shape[0], x.shape[1]
    for l in range(DEPTH):
        h = rmsnorm(x, norm1_g[l])
        z = h @ w_in[l]
        q = z[..., :Q_COLS].reshape(B, S, N_HEADS, HEAD_DIM)
        k = z[..., Q_COLS:Q_COLS + KV_COLS].reshape(B, S, N_KV_HEADS, HEAD_DIM)
        v = z[..., Q_COLS + KV_COLS:Q_COLS + 2 * KV_COLS].reshape(B, S, N_KV_HEADS, HEAD_DIM)
        zg = z[..., Q_COLS + 2 * KV_COLS:]
        attn = windowed_gqa(q, k, v, attn_sink[l])
        gm = chunked_spatial_gating(zg, gmlp_v_norm_g[l], gmlp_ws[l], gmlp_b[l])
        mix = jnp.concatenate([rmsnorm(attn, attn_out_norm_g[l]),
                               rmsnorm(gm, gmlp_out_norm_g[l])], axis=-1)
        x = x + mix @ w_out[l]
        x = x + conv_ffn(rmsnorm(x, norm2_g[l]), w_up[l], conv_w[l], conv_b[l], w_down[l])
    return rmsnorm(x, final_g)
```

```python
import functools
import math

import jax
import jax.numpy as jnp
from jax import lax
from jax.experimental import pallas as pl
from jax.experimental.pallas import tpu as pltpu

F32 = jnp.float32
BF16 = jnp.bfloat16

EPS = 1e-6
HEAD_DIM = 128
N_HEADS = 8
N_KV_HEADS = 2
GQA_GROUP = N_HEADS // N_KV_HEADS
ATTN_BLOCK = 128
GMLP_GROUP_DIM = 128
GMLP_CHUNK = 128
CONV_WIDTH = 3

V7X_VMEM_BYTES = 64 * 1024 * 1024
F32_SUBLANES = 8
BF16_SUBLANES = 16

IN_PROJ_ROWS = 512
MIX_ROWS = 256
FFN_ROWS = 512
FFN_COLS = 512
VMEM_LIMIT_BYTES = 56 * 1024 * 1024


def _rms_scale(x):
    return lax.rsqrt(jnp.mean(x * x, axis=-1, keepdims=True) + EPS)


def _resident(shape):
    nd = len(shape)
    return pl.BlockSpec(shape, lambda *_: (0,) * nd, pipeline_mode=pl.Buffered(1))


def _in_proj_kernel(x_ref, g_ref, w_ref, q_ref, kv_ref, zg_ref, *, q_cols, kv_cols):
    x = x_ref[...]
    h = ((x * _rms_scale(x)) * g_ref[...]).astype(BF16)
    q_ref[...] = jnp.dot(h, w_ref[:, :q_cols], preferred_element_type=F32).astype(BF16)
    kv_ref[...] = jnp.dot(h, w_ref[:, q_cols:q_cols + kv_cols],
                          preferred_element_type=F32).astype(BF16)
    zg_ref[...] = jnp.dot(h, w_ref[:, q_cols + kv_cols:], preferred_element_type=F32)


def _in_proj(x, g, w, *, q_cols, kv_cols):
    t, d = x.shape
    n_cols = w.shape[1]
    zg_cols = n_cols - q_cols - kv_cols
    tm = IN_PROJ_ROWS
    assert t % tm == 0
    row = lambda c: pl.BlockSpec((tm, c), lambda i: (i, 0))
    return pl.pallas_call(
        functools.partial(_in_proj_kernel, q_cols=q_cols, kv_cols=kv_cols),
        out_shape=(jax.ShapeDtypeStruct((t, q_cols), BF16),
                   jax.ShapeDtypeStruct((t, kv_cols), BF16),
                   jax.ShapeDtypeStruct((t, zg_cols), F32)),
        grid=(t // tm,),
        in_specs=[row(d), _resident((1, d)), _resident((d, n_cols))],
        out_specs=(row(q_cols), row(kv_cols), row(zg_cols)),
        compiler_params=pltpu.CompilerParams(
            dimension_semantics=("parallel",), vmem_limit_bytes=VMEM_LIMIT_BYTES),
        name="in_proj",
    )(x, g, w)


def _mix_kernel(sink_ref, x_ref, q_ref, kvp_ref, kvc_ref, kvn_ref, zg_ref,
                vng_ref, ws_ref, bt_ref, ang_ref, gng_ref, wout_ref,
                o_ref, kv_scr, bias_scr, attn_scr, gm_scr, *, tiles_per_seq):
    i = pl.program_id(0)
    tq = x_ref.shape[0]
    n_blk = tq // ATTN_BLOCK
    band = 3 * ATTN_BLOCK
    kcols = N_KV_HEADS * HEAD_DIM
    d_attn = N_HEADS * HEAD_DIM
    d_gmlp = gm_scr.shape[1]
    n_groups = d_gmlp // GMLP_GROUP_DIM

    @pl.when(i == 0)
    def _():
        qi = lax.broadcasted_iota(jnp.int32, (ATTN_BLOCK, band), 0)
        sj = lax.broadcasted_iota(jnp.int32, (ATTN_BLOCK, band), 1)
        dist = jnp.abs(sj - ATTN_BLOCK - qi)
        in_band = dist <= ATTN_BLOCK
        dist = dist.astype(F32)
        for h in range(N_HEADS):
            slope = 2.0 ** (-8.0 * (h + 1) / N_HEADS)
            bias_scr[h] = jnp.where(in_band, -slope * dist, -jnp.inf)

    kv_scr[0:ATTN_BLOCK] = kvp_ref[...]
    kv_scr[ATTN_BLOCK:ATTN_BLOCK + tq] = kvc_ref[...]
    kv_scr[ATTN_BLOCK + tq:] = kvn_ref[...]

    tile_in_seq = i % tiles_per_seq
    col = lax.broadcasted_iota(jnp.int32, (1, band), 1)
    scale = HEAD_DIM ** -0.5

    for n in range(n_blk):
        r0 = n * ATTN_BLOCK
        lo = jnp.where(jnp.logical_and(tile_in_seq == 0, n == 0), ATTN_BLOCK, 0)
        hi = jnp.where(jnp.logical_and(tile_in_seq == tiles_per_seq - 1, n == n_blk - 1),
                       2 * ATTN_BLOCK, band)
        colmask = jnp.where(jnp.logical_and(col >= lo, col < hi), 0.0, -jnp.inf).astype(F32)
        for g in range(N_KV_HEADS):
            kb = kv_scr[r0:r0 + band, g * HEAD_DIM:(g + 1) * HEAD_DIM]
            vb = kv_scr[r0:r0 + band, kcols + g * HEAD_DIM:kcols + (g + 1) * HEAD_DIM]
            heads = [g * GQA_GROUP + hh for hh in range(GQA_GROUP)]
            qg = jnp.concatenate(
                [q_ref[r0:r0 + ATTN_BLOCK, h * HEAD_DIM:(h + 1) * HEAD_DIM] for h in heads],
                axis=0)
            s = lax.dot_general(qg, kb, (((1,), (1,)), ((), ())),
                                preferred_element_type=F32)
            es, denoms = [], []
            for hh, h in enumerate(heads):
                sh = s[hh * ATTN_BLOCK:(hh + 1) * ATTN_BLOCK] * scale + bias_scr[h] + colmask
                sink = sink_ref[h]
                m = jnp.maximum(jnp.max(sh, axis=-1, keepdims=True), sink)
                e = jnp.exp(sh - m)
                denoms.append(jnp.sum(e, axis=-1, keepdims=True) + jnp.exp(sink - m))
                es.append(e.astype(BF16))
            pv = jnp.dot(jnp.concatenate(es, axis=0), vb, preferred_element_type=F32)
            for hh, h in enumerate(heads):
                attn_scr[r0:r0 + ATTN_BLOCK, h * HEAD_DIM:(h + 1) * HEAD_DIM] = (
                    pv[hh * ATTN_BLOCK:(hh + 1) * ATTN_BLOCK] / denoms[hh])

    c0 = math.sqrt(2.0 / math.pi)
    for c in range(tq // GMLP_CHUNK):
        r0 = c * GMLP_CHUNK
        for h in range(n_groups):
            cs = slice(h * GMLP_GROUP_DIM, (h + 1) * GMLP_GROUP_DIM)
            zu = zg_ref[r0:r0 + GMLP_CHUNK, cs]
            zv = zg_ref[r0:r0 + GMLP_CHUNK, d_gmlp + h * GMLP_GROUP_DIM:
                        d_gmlp + (h + 1) * GMLP_GROUP_DIM]
            u = 0.5 * zu * (1.0 + jnp.tanh(c0 * (zu + 0.044715 * (zu * zu * zu))))
            v = 0.5 * zv * (1.0 + jnp.tanh(c0 * (zv + 0.044715 * (zv * zv * zv))))
            vn = ((v * _rms_scale(v)) * vng_ref[:, cs]).astype(BF16)
            wv = jnp.dot(ws_ref[h], vn, preferred_element_type=F32) + bt_ref[:, h:h + 1]
            gm_scr[r0:r0 + GMLP_CHUNK, cs] = u * wv

    a = attn_scr[...]
    an = ((a * _rms_scale(a)) * ang_ref[...]).astype(BF16)
    gm = gm_scr[...]
    gn = ((gm * _rms_scale(gm)) * gng_ref[...]).astype(BF16)
    y = jnp.dot(an, wout_ref[:d_attn, :], preferred_element_type=F32)
    y = y + jnp.dot(gn, wout_ref[d_attn:, :], preferred_element_type=F32)
    o_ref[...] = x_ref[...] + y


def _mix(x, q, kv, zg, sink, vng, ws, bt, ang, gng, wout, *, seq_len):
    t, d = x.shape
    tq = MIX_ROWS
    assert seq_len % tq == 0 and tq % ATTN_BLOCK == 0 and tq % GMLP_CHUNK == 0
    blk_per_tile = tq // ATTN_BLOCK
    n_kv_blocks = t // ATTN_BLOCK
    d_attn = q.shape[1]
    kv_cols = kv.shape[1]
    d_gmlp = zg.shape[1] // 2
    row = lambda c: pl.BlockSpec((tq, c), lambda i: (i, 0))
    kv_prev = pl.BlockSpec((ATTN_BLOCK, kv_cols),
                           lambda i: (jnp.maximum(i * blk_per_tile - 1, 0), 0))
    kv_next = pl.BlockSpec((ATTN_BLOCK, kv_cols),
                           lambda i: (jnp.minimum((i + 1) * blk_per_tile, n_kv_blocks - 1), 0))
    return pl.pallas_call(
        functools.partial(_mix_kernel, tiles_per_seq=seq_len // tq),
        out_shape=jax.ShapeDtypeStruct((t, d), F32),
        grid=(t // tq,),
        in_specs=[pl.BlockSpec(memory_space=pltpu.SMEM),
                  row(d), row(d_attn), kv_prev, row(kv_cols), kv_next, row(2 * d_gmlp),
                  _resident(vng.shape), _resident(ws.shape), _resident(bt.shape),
                  _resident(ang.shape), _resident(gng.shape), _resident(wout.shape)],
        out_specs=row(d),
        scratch_shapes=[pltpu.VMEM((tq + 2 * ATTN_BLOCK, kv_cols), BF16),
                        pltpu.VMEM((N_HEADS, ATTN_BLOCK, 3 * ATTN_BLOCK), F32),
                        pltpu.VMEM((tq, d_attn), F32),
                        pltpu.VMEM((tq, d_gmlp), F32)],
        compiler_params=pltpu.CompilerParams(
            dimension_semantics=("arbitrary",), vmem_limit_bytes=VMEM_LIMIT_BYTES),
        name="mix",
    )(sink, x, q, kv, kv, kv, zg, vng, ws, bt, ang, gng, wout)


def _ffn_kernel(xp_ref, xc_ref, xn_ref, g2_ref, wg_ref, wu_ref, cwg_ref, cwu_ref,
                cbg_ref, cbu_ref, wd_ref, fg_ref, o_ref, h_scr, *, tiles_per_seq, final_norm):
    i = pl.program_id(0)
    j = pl.program_id(1)
    tm = xc_ref.shape[0]

    @pl.when(j == 0)
    def _():
        xc = xc_ref[...]
        h_scr[0:tm] = ((xc * _rms_scale(xc)) * g2_ref[...]).astype(BF16)
        xh = jnp.concatenate([xp_ref[...], xn_ref[...]], axis=0)
        h_scr[tm:] = ((xh * _rms_scale(xh)) * g2_ref[...]).astype(BF16)
        o_ref[...] = xc

    tile_in_seq = i % tiles_per_seq
    has_prev = tile_in_seq > 0
    has_next = tile_in_seq < tiles_per_seq - 1
    h = h_scr[...]
    rows8 = lax.broadcasted_iota(jnp.int32, (F32_SUBLANES, 1), 0)

    def conv(w_ref, cw_ref, cb_ref):
        a = jnp.dot(h, w_ref[...], preferred_element_type=F32)
        cur = a[:tm]
        before = jnp.where(has_prev, a[tm + F32_SUBLANES - 1:tm + F32_SUBLANES], 0.0)
        after = jnp.where(has_next, a[tm + F32_SUBLANES:tm + F32_SUBLANES + 1], 0.0)
        up = pltpu.roll(cur, 1, 0)
        up = jnp.concatenate(
            [jnp.where(rows8 == 0, before, up[:F32_SUBLANES]), up[F32_SUBLANES:]], axis=0)
        dn = pltpu.roll(cur, tm - 1, 0)
        dn = jnp.concatenate(
            [dn[:tm - F32_SUBLANES],
             jnp.where(rows8 == F32_SUBLANES - 1, after, dn[tm - F32_SUBLANES:])], axis=0)
        return cw_ref[0:1] * up + cw_ref[1:2] * cur + cw_ref[2:3] * dn + cb_ref[...]

    gate = conv(wg_ref, cwg_ref, cbg_ref)
    up_ = conv(wu_ref, cwu_ref, cbu_ref)
    act = (gate * (1.0 / (1.0 + jnp.exp(-gate))) * up_).astype(BF16)
    o_ref[...] += jnp.dot(act, wd_ref[...], preferred_element_type=F32)

    if final_norm:
        @pl.when(j == pl.num_programs(1) - 1)
        def _():
            y = o_ref[...]
            o_ref[...] = (y * _rms_scale(y)) * fg_ref[...]


def _conv_ffn(x, g2, w_up, conv_w, conv_b, w_down, final_g, *, seq_len, final_norm):
    t, d = x.shape
    d_ff = w_down.shape[0]
    tm, fc = FFN_ROWS, FFN_COLS
    assert seq_len % tm == 0 and d_ff % fc == 0 and tm % BF16_SUBLANES == 0
    nj = d_ff // fc
    halo = F32_SUBLANES
    n_halo_blocks = t // halo
    x_prev = pl.BlockSpec((halo, d), lambda i, j: (jnp.maximum(i * (tm // halo) - 1, 0), 0))
    x_next = pl.BlockSpec(
        (halo, d), lambda i, j: (jnp.minimum((i + 1) * (tm // halo), n_halo_blocks - 1), 0))
    gate_cols = lambda r: pl.BlockSpec((r, fc), lambda i, j: (0, j))
    up_cols = lambda r: pl.BlockSpec((r, fc), lambda i, j: (0, nj + j))
    return pl.pallas_call(
        functools.partial(_ffn_kernel, tiles_per_seq=seq_len // tm, final_norm=final_norm),
        out_shape=jax.ShapeDtypeStruct((t, d), F32),
        grid=(t // tm, nj),
        in_specs=[x_prev, pl.BlockSpec((tm, d), lambda i, j: (i, 0)), x_next,
                  _resident((1, d)),
                  gate_cols(d), up_cols(d),
                  gate_cols(CONV_WIDTH), up_cols(CONV_WIDTH),
                  gate_cols(1), up_cols(1),
                  pl.BlockSpec((fc, d), lambda i, j: (j, 0)),
                  _resident((1, d))],
        out_specs=pl.BlockSpec((tm, d), lambda i, j: (i, 0)),
        scratch_shapes=[pltpu.VMEM((tm + 2 * halo, d), BF16)],
        compiler_params=pltpu.CompilerParams(
            dimension_semantics=("parallel", "arbitrary"), vmem_limit_bytes=VMEM_LIMIT_BYTES),
        name="conv_ffn",
    )(x, x, x, g2, w_up, w_up, conv_w, conv_w, conv_b, conv_b, w_down, final_g)


def kernel(x, norm1_g, w_in, gmlp_v_norm_g, gmlp_ws, gmlp_b, attn_sink, attn_out_norm_g,
           gmlp_out_norm_g, w_out, norm2_g, w_up, conv_w, conv_b, w_down, final_g):
    b, s, d = x.shape
    depth = w_in.shape[0]
    q_cols = N_HEADS * HEAD_DIM
    kv_cols = 2 * N_KV_HEADS * HEAD_DIM
    xf = x.reshape(b * s, d)
    row = lambda v: v.reshape(1, -1)
    for l in range(depth):
        q, kv, zg = _in_proj(xf, row(norm1_g[l]), w_in[l].astype(BF16),
                             q_cols=q_cols, kv_cols=kv_cols)
        xf = _mix(xf, q, kv, zg, attn_sink[l], row(gmlp_v_norm_g[l]),
                  gmlp_ws[l].astype(BF16), gmlp_b[l].T, row(attn_out_norm_g[l]),
                  row(gmlp_out_norm_g[l]), w_out[l].astype(BF16), seq_len=s)
        xf = _conv_ffn(xf, row(norm2_g[l]), w_up[l].astype(BF16), conv_w[l],
                       row(conv_b[l]), w_down[l].astype(BF16), row(final_g),
                       seq_len=s, final_norm=(l == depth - 1))
    return xf.reshape(b, s, d)
```
